```python
import jax, jax.numpy as jnp
from jax import lax
import numpy as np


D_MODEL = 2048
BATCH = 1
SEQ = 8192
DEPTH = 4

GRID_W = 64
CTX_LEN = 256
N_MIXERS = 3
Q_BLOCK = 128
EPS = 1e-6
ROPE_THETA = 10000.0
D_FF = 4 * D_MODEL

A_HEADS = 16
A_KV_HEADS = 4
A_HEAD_DIM = D_MODEL // A_HEADS
A_GROUP = A_HEADS // A_KV_HEADS

CONV_WIDTH = 3

C_HEADS = 16
C_NOPE = 128
C_ROPE = 64
C_V = 128
C_Q_RANK = 512
C_KV_RANK = 512

N_A = (DEPTH - 0 + N_MIXERS - 1) // N_MIXERS
N_B = (DEPTH - 1 + N_MIXERS - 1) // N_MIXERS
N_C = (DEPTH - 2 + N_MIXERS - 1) // N_MIXERS

kernel_name = 'hybrid_dit_gqa_shortconv_mla'


def rms_norm(x, g):
    xf = x.astype(jnp.float32)
    y = xf * lax.rsqrt(jnp.mean(xf * xf, axis=-1, keepdims=True) + EPS)
    return (y * g.astype(jnp.float32)).astype(x.dtype)


def modulate(h, shift, scale):
    return h * (1.0 + scale) + shift


def axial_rope_tables(row, col, rot_dim):
    n_axis = rot_dim // 4
    inv = ROPE_THETA ** (-jnp.arange(n_axis, dtype=jnp.float32) / n_axis)
    ang = jnp.concatenate([row.astype(jnp.float32)[:, None] * inv,
                           col.astype(jnp.float32)[:, None] * inv], axis=-1)
    return jnp.cos(ang), jnp.sin(ang)


def apply_rope(x, cos, sin):
    half = x.shape[-1] // 2
    x1, x2 = x[..., :half], x[..., half:]
    cs = cos[None, :, None, :].astype(x.dtype)
    sn = sin[None, :, None, :].astype(x.dtype)
    return jnp.concatenate([x1 * cs - x2 * sn, x1 * sn + x2 * cs], axis=-1)


def blocked_attention(q, k, v, scale):
    b, s, kh, g, dk = q.shape
    dv = v.shape[-1]
    nb = s // Q_BLOCK
    qb = q.reshape(b, nb, Q_BLOCK, kh, g, dk).transpose(1, 0, 2, 3, 4, 5)

    def one_block(qblk):
        sc = jnp.einsum('bqhgd,bkhd->bhgqk', qblk, k).astype(jnp.float32) * scale
        p = jax.nn.softmax(sc, axis=-1).astype(v.dtype)
        return jnp.einsum('bhgqk,bkhd->bqhgd', p, v)

    o = lax.map(one_block, qb)
    return o.transpose(1, 0, 2, 3, 4, 5).reshape(b, s, kh, g, dv)


def gqa_mixer(h, hc, w_qkv, q_gain, k_gain, w_o, cos, sin, ctx_out):
    def proj(z):
        b, l, _ = z.shape
        qkv = z @ w_qkv
        q = qkv[..., :A_HEADS * A_HEAD_DIM].reshape(b, l, A_HEADS, A_HEAD_DIM)
        k = qkv[..., A_HEADS * A_HEAD_DIM:(A_HEADS + A_KV_HEADS) * A_HEAD_DIM].reshape(b, l, A_KV_HEADS, A_HEAD_DIM)
        v = qkv[..., (A_HEADS + A_KV_HEADS) * A_HEAD_DIM:].reshape(b, l, A_KV_HEADS, A_HEAD_DIM)
        return rms_norm(q, q_gain), rms_norm(k, k_gain), v

    q, k, v = proj(h)
    qc, kc, vc = proj(hc)
    q = apply_rope(q, cos, sin)
    k = apply_rope(k, cos, sin)
    b, s = h.shape[0], h.shape[1]
    scale = A_HEAD_DIM ** -0.5
    k_all = jnp.concatenate([k, kc], axis=1)
    v_all = jnp.concatenate([v, vc], axis=1)
    o = blocked_attention(q.reshape(b, s, A_KV_HEADS, A_GROUP, A_HEAD_DIM), k_all, v_all, scale)
    y = o.reshape(b, s, A_HEADS * A_HEAD_DIM) @ w_o
    yc = None
    if ctx_out:
        lc = hc.shape[1]
        oc = blocked_attention(qc.reshape(b, lc, A_KV_HEADS, A_GROUP, A_HEAD_DIM), kc, vc, scale)
        yc = oc.reshape(b, lc, A_HEADS * A_HEAD_DIM) @ w_o
    return y, yc


def depthwise_conv_centred(z, w):
    pad = (CONV_WIDTH - 1) // 2
    return lax.conv_general_dilated(
        z, w[:, None, :].astype(z.dtype), window_strides=(1,), padding=[(pad, pad)],
        dimension_numbers=('NWC', 'WIO', 'NWC'), feature_group_count=z.shape[-1])


def short_conv_mixer(h, hc, w_in, conv_w, w_out, ctx_out):
    def run(z):
        bg, cg, u = jnp.split(z @ w_in, 3, axis=-1)
        return (bg * depthwise_conv_centred(cg * u, conv_w)) @ w_out
    y = run(h)
    yc = run(hc) if ctx_out else None
    return y, yc


def mla_mixer(h, hc, w_dq, q_gain, w_uq, w_dkv, kv_gain, w_ukv, w_o, cos, sin, ctx_out):
    def proj(z):
        b, l, _ = z.shape
        q = (rms_norm(z @ w_dq, q_gain) @ w_uq).reshape(b, l, C_HEADS, C_NOPE + C_ROPE)
        ckv_kr = z @ w_dkv
        ckv = rms_norm(ckv_kr[..., :C_KV_RANK], kv_gain)
        k_rope = ckv_kr[..., C_KV_RANK:][:, :, None, :]
        kv = (ckv @ w_ukv).reshape(b, l, C_HEADS, C_NOPE + C_V)
        return q[..., :C_NOPE], q[..., C_NOPE:], kv[..., :C_NOPE], k_rope, kv[..., C_NOPE:]

    def assemble(q_nope, q_rope, k_nope, k_rope, v):
        b, l = q_nope.shape[0], q_nope.shape[1]
        qf = jnp.concatenate([q_nope, q_rope], axis=-1)
        kf = jnp.concatenate([k_nope, jnp.broadcast_to(k_rope, (b, l, C_HEADS, C_ROPE))], axis=-1)
        return qf, kf, v

    qn, qr, kn, kr, v = proj(h)
    qr = apply_rope(qr, cos, sin)
    kr = apply_rope(kr, cos, sin)
    q, k, v = assemble(qn, qr, kn, kr, v)
    qc, kc, vc = assemble(*proj(hc))
    b, s = h.shape[0], h.shape[1]
    scale = (C_NOPE + C_ROPE) ** -0.5
    k_all = jnp.concatenate([k, kc], axis=1)
    v_all = jnp.concatenate([v, vc], axis=1)
    o = blocked_attention(q[:, :, :, None, :], k_all, v_all, scale)
    y = o.reshape(b, s, C_HEADS * C_V) @ w_o
    yc = None
    if ctx_out:
        lc = hc.shape[1]
        oc = blocked_attention(qc[:, :, :, None, :], kc, vc, scale)
        yc = oc.reshape(b, lc, C_HEADS * C_V) @ w_o
    return y, yc


def sq_relu_mlp(h, w1, w2):
    return jnp.square(jax.nn.relu(h @ w1)) @ w2


def setup_inputs(seed: int = 0) -> dict:
    key = jax.random.key(seed)
    ks = jax.random.split(key, 32)
    f32 = jnp.float32

    def nrm(k, shape, fan_in, gain=1.0):
        return jax.random.normal(k, shape, f32) * (gain * fan_in ** -0.5)

    def gain_(k, shape):
        return 1.0 + 0.02 * jax.random.normal(k, shape, f32)

    d = D_MODEL
    qkv_w = (A_HEADS + 2 * A_KV_HEADS) * A_HEAD_DIM
    return {
        'x': jax.random.normal(ks[0], (BATCH, SEQ, d), f32),
        'c': jax.random.normal(ks[1], (BATCH, d), f32),
        'ctx': jax.random.normal(ks[2], (BATCH, CTX_LEN, d), f32),
        'c_ctx': jax.random.normal(ks[3], (d,), f32),
        'w_ada': nrm(ks[4], (DEPTH, d, 6 * d), d, 0.5),
        'b_ada': 0.02 * jax.random.normal(ks[5], (DEPTH, 6 * d), f32),
        'norm1': gain_(ks[6], (DEPTH, d)),
        'norm2': gain_(ks[7], (DEPTH, d)),
        'w_mlp1': nrm(ks[8], (DEPTH, d, D_FF), d),
        'w_mlp2': nrm(ks[9], (DEPTH, D_FF, d), D_FF),
        'a_w_qkv': nrm(ks[10], (N_A, d, qkv_w), d),
        'a_q_norm': gain_(ks[11], (N_A, A_HEAD_DIM)),
        'a_k_norm': gain_(ks[12], (N_A, A_HEAD_DIM)),
        'a_w_o': nrm(ks[13], (N_A, A_HEADS * A_HEAD_DIM, d), A_HEADS * A_HEAD_DIM),
        'b_w_in': nrm(ks[14], (N_B, d, 3 * d), d),
        'b_conv': nrm(ks[15], (N_B, CONV_WIDTH, d), CONV_WIDTH),
        'b_w_out': nrm(ks[16], (N_B, d, d), d),
        'c_w_dq': nrm(ks[17], (N_C, d, C_Q_RANK), d),
        'c_q_norm': gain_(ks[18], (N_C, C_Q_RANK)),
        'c_w_uq': nrm(ks[19], (N_C, C_Q_RANK, C_HEADS * (C_NOPE + C_ROPE)), C_Q_RANK),
        'c_w_dkv': nrm(ks[20], (N_C, d, C_KV_RANK + C_ROPE), d),
        'c_kv_norm': gain_(ks[21], (N_C, C_KV_RANK)),
        'c_w_ukv': nrm(ks[22], (N_C, C_KV_RANK, C_HEADS * (C_NOPE + C_V)), C_KV_RANK),
        'c_w_o': nrm(ks[23], (N_C, C_HEADS * C_V, d), C_HEADS * C_V),
        'final_norm': gain_(ks[24], (d,)),
    }


def reference(x, c, ctx, c_ctx, w_ada, b_ada, norm1, norm2, w_mlp1, w_mlp2,
              a_w_qkv, a_q_norm, a_k_norm, a_w_o, b_w_in, b_conv, b_w_out,
              c_w_dq, c_q_norm, c_w_uq, c_w_dkv, c_kv_norm, c_w_ukv, c_w_o, final_norm):
    seq = x.shape[1]
    rows = seq // GRID_W
    row = jnp.repeat(jnp.arange(rows, dtype=jnp.int32), GRID_W)
    col = jnp.tile(jnp.arange(GRID_W, dtype=jnp.int32), rows)
    cos_a, sin_a = axial_rope_tables(row, col, A_HEAD_DIM)
    cos_c, sin_c = axial_rope_tables(row, col, C_ROPE)

    silu_c = jax.nn.silu(c)
    silu_cc = jax.nn.silu(c_ctx)
    xc = ctx
    for i in range(DEPTH):
        ctx_out = i < DEPTH - 1
        mixer, j = i % N_MIXERS, i // N_MIXERS
        sh1, sc1, g1, sh2, sc2, g2 = jnp.split((silu_c @ w_ada[i] + b_ada[i])[:, None, :], 6, axis=-1)
        csh1, csc1, cg1, csh2, csc2, cg2 = jnp.split(silu_cc @ w_ada[i] + b_ada[i], 6, axis=-1)
        h = modulate(rms_norm(x, norm1[i]), sh1, sc1)
        hc = modulate(rms_norm(xc, norm1[i]), csh1, csc1)
        if mixer == 0:
            y, yc = gqa_mixer(h, hc, a_w_qkv[j], a_q_norm[j], a_k_norm[j], a_w_o[j], cos_a, sin_a, ctx_out)
        elif mixer == 1:
            y, yc = short_conv_mixer(h, hc, b_w_in[j], b_conv[j], b_w_out[j], ctx_out)
        else:
            y, yc = mla_mixer(h, hc, c_w_dq[j], c_q_norm[j], c_w_uq[j], c_w_dkv[j], c_kv_norm[j],
                              c_w_ukv[j], c_w_o[j], cos_c, sin_c, ctx_out)
        x = x + g1 * y
        x = x + g2 * sq_relu_mlp(modulate(rms_norm(x, norm2[i]), sh2, sc2), w_mlp1[i], w_mlp2[i])
        if ctx_out:
            xc = xc + cg1 * yc
            xc = xc + cg2 * sq_relu_mlp(modulate(rms_norm(xc, norm2[i]), csh2, csc2), w_mlp1[i], w_mlp2[i])
    return rms_norm(x, final_norm)
```

```python
import functools
import math

import jax
import jax.numpy as jnp
from jax import lax
from jax.experimental import pallas as pl
from jax.experimental.pallas import tpu as pltpu

F32 = jnp.float32
BF16 = jnp.bfloat16

EPS = 1e-6
ROPE_THETA = 10000.0
GRID_W = 64
N_MIXERS = 3

A_HEADS, A_KV_HEADS, A_HEAD_DIM = 16, 4, 128
A_GROUP = A_HEADS // A_KV_HEADS
CONV_WIDTH = 3
C_HEADS, C_NOPE, C_ROPE, C_V = 16, 128, 64, 128
C_QK_PAD = 256
LANES = 128
SUBLANES = 8
LOG2E = math.log2(math.e)

VMEM_LIMIT = 56 * 1024 * 1024


def _cparams(sem):
    return pltpu.CompilerParams(dimension_semantics=sem, vmem_limit_bytes=VMEM_LIMIT)


def _dot(a, b):
    return jnp.dot(a, b, preferred_element_type=F32)


def _const_spec(shape):
    nd = len(shape)
    return pl.BlockSpec(shape, lambda *_: (0,) * nd, pipeline_mode=pl.Buffered(1))


def _row_vec(mod_ref, k, row0, tm, seq):
    lat = mod_ref[0, k:k + 1, :]
    ctx = mod_ref[1, k:k + 1, :]
    rows = row0 + lax.broadcasted_iota(jnp.int32, (tm, 1), 0)
    return jnp.where(rows >= seq, ctx, lat)


def _rms(x, gain):
    return x * lax.rsqrt(jnp.mean(x * x, axis=-1, keepdims=True) + EPS) * gain


def _norm_mod(x, gain, mod_ref, k_shift, k_scale, row0, tm, seq):
    h = _rms(x, gain)
    return h * (1.0 + _row_vec(mod_ref, k_scale, row0, tm, seq)) + _row_vec(mod_ref, k_shift, row0, tm, seq)


def _ada_kernel(c_ref, w_ref, b_ref, o_ref):
    cv = c_ref[...]
    s = cv / (1.0 + jnp.exp(-cv))
    s_hi = s.astype(BF16)
    s_lo = (s - s_hi.astype(F32)).astype(BF16)
    w = w_ref[0]
    w_hi = w.astype(BF16)
    w_lo = (w - w_hi.astype(F32)).astype(BF16)
    o_ref[0] = _dot(s_hi, w_hi) + _dot(s_lo, w_hi) + _dot(s_hi, w_lo) + b_ref[0]


def _ada(c8, w_ada, b_ada, tn=1024):
    depth, d, n = w_ada.shape
    return pl.pallas_call(
        _ada_kernel,
        grid=(depth, n // tn),
        in_specs=[
            _const_spec((SUBLANES, d)),
            pl.BlockSpec((1, d, tn), lambda i, j: (i, 0, j)),
            pl.BlockSpec((1, 1, tn), lambda i, j: (i, 0, j)),
        ],
        out_specs=pl.BlockSpec((1, SUBLANES, tn), lambda i, j: (i, 0, j)),
        out_shape=jax.ShapeDtypeStruct((depth, SUBLANES, n), F32),
        compiler_params=_cparams(("arbitrary", "arbitrary")),
        name="ada",
    )(c8, w_ada, b_ada.reshape(depth, 1, n))


def _qkv_kernel(x_ref, mod_ref, n1_ref, w_ref, gain_ref, cos_ref, sin_ref, o_ref, h_sc, *, tm, seq, n_norm_tiles):
    i = pl.program_id(0)
    j = pl.program_id(1)

    @pl.when(j == 0)
    def _():
        h_sc[...] = _norm_mod(x_ref[...], n1_ref[...], mod_ref, 0, 1, i * tm, tm, seq).astype(BF16)

    acc = _dot(h_sc[...], w_ref[...])
    heads = acc.shape[1] // A_HEAD_DIM

    @pl.when(j < n_norm_tiles)
    def _():
        g = gain_ref[0]
        cs = cos_ref[...]
        sn = sin_ref[...]
        for hh in range(heads):
            sl = slice(hh * A_HEAD_DIM, (hh + 1) * A_HEAD_DIM)
            y = _rms(acc[:, sl], g)
            y = y * cs + pltpu.roll(y, A_HEAD_DIM // 2, axis=1) * sn
            o_ref[:, sl] = y.astype(BF16)

    @pl.when(j >= n_norm_tiles)
    def _():
        o_ref[...] = acc.astype(BF16)


def _gqa_qkv(xall, mods, n1, w_qkv, gains, cos, sin, *, seq, tm=768, tn=512):
    t, d = xall.shape
    n = w_qkv.shape[1]
    n_norm_tiles = (A_HEADS + A_KV_HEADS) * A_HEAD_DIM // tn
    return pl.pallas_call(
        functools.partial(_qkv_kernel, tm=tm, seq=seq, n_norm_tiles=n_norm_tiles),
        grid=(t // tm, n // tn),
        in_specs=[
            pl.BlockSpec((tm, d), lambda i, j: (i, 0)),
            _const_spec(mods.shape),
            _const_spec((1, d)),
            pl.BlockSpec((d, tn), lambda i, j: (0, j)),
            pl.BlockSpec((1, 1, A_HEAD_DIM), lambda i, j: (j, 0, 0)),
            pl.BlockSpec((tm, A_HEAD_DIM), lambda i, j: (i, 0)),
            pl.BlockSpec((tm, A_HEAD_DIM), lambda i, j: (i, 0)),
        ],
        out_specs=pl.BlockSpec((tm, tn), lambda i, j: (i, j)),
        out_shape=jax.ShapeDtypeStruct((t, n), BF16),
        scratch_shapes=[pltpu.VMEM((tm, d), BF16)],
        compiler_params=_cparams(("arbitrary", "arbitrary")),
        name="gqa_qkv",
    )(xall, mods, n1, w_qkv, gains, cos, sin)


def _flash_kernel(q_ref, k_ref, v_ref, o_ref, *, group, dk, dv, tq, tk, seq, n_ctx):
    qi = pl.program_id(1)
    if group > 1:
        q = jnp.concatenate([q_ref[:, g * dk:(g + 1) * dk] for g in range(group)], axis=0)
    else:
        q = q_ref[...]
    m_rows = group * tq

    def step(k, v, carry):
        m, l, acc = carry
        s = lax.dot_general(q, k, (((1,), (1,)), ((), ())), preferred_element_type=F32)
        m_new = jnp.maximum(m, jnp.max(s, axis=1, keepdims=True))
        alpha = jnp.exp2(m - m_new)
        p = jnp.exp2(s - m_new)
        l = alpha * l + jnp.sum(p, axis=1, keepdims=True)
        acc = alpha * acc + _dot(p.astype(BF16), v)
        return m_new, l, acc

    def latent_chunk(c, carry):
        start = pl.multiple_of(c * tk, tk)
        return step(k_ref[pl.ds(start, tk), :], v_ref[pl.ds(start, tk), :], carry)

    init = (jnp.full((m_rows, 1), -jnp.inf, F32), jnp.zeros((m_rows, 1), F32), jnp.zeros((m_rows, dv), F32))
    n_latent_chunks = jnp.where(qi * tq >= seq, 0, seq // tk)
    carry = lax.fori_loop(0, n_latent_chunks, latent_chunk, init)
    _, l, acc = step(k_ref[seq:seq + n_ctx, :], v_ref[seq:seq + n_ctx, :], carry)
    o = acc / l
    for g in range(group):
        o_ref[:, g * dv:(g + 1) * dv] = o[g * tq:(g + 1) * tq].astype(BF16)


def _flash(q_arr, k_arr, v_arr, *, n_kv, group, dk, dv, q_rows, q_col0, k_col0, v_col0, seq, n_ctx, tq=256, tk=1024):
    t = k_arr.shape[0]
    assert q_col0 % (group * dk) == 0 and k_col0 % dk == 0 and v_col0 % dv == 0
    return pl.pallas_call(
        functools.partial(_flash_kernel, group=group, dk=dk, dv=dv, tq=tq, tk=tk, seq=seq, n_ctx=n_ctx),
        grid=(n_kv, q_rows // tq),
        in_specs=[
            pl.BlockSpec((tq, group * dk), lambda h, i: (i, q_col0 // (group * dk) + h)),
            pl.BlockSpec((t, dk), lambda h, i: (0, k_col0 // dk + h)),
            pl.BlockSpec((t, dv), lambda h, i: (0, v_col0 // dv + h)),
        ],
        out_specs=pl.BlockSpec((tq, group * dv), lambda h, i: (i, h)),
        out_shape=jax.ShapeDtypeStruct((q_rows, n_kv * group * dv), BF16),
        compiler_params=_cparams(("arbitrary", "arbitrary")),
        name="flash",
    )(q_arr, k_arr, v_arr)


def _oproj_kernel(o_ref, w_ref, x_ref, mod_ref, out_ref, *, tm, seq):
    row0 = pl.program_id(0) * tm
    y = _dot(o_ref[...], w_ref[...])
    out_ref[...] = x_ref[...] + _row_vec(mod_ref, 2, row0, tm, seq) * y


def _oproj(o, w_o, xall, mods, *, rows, seq, tm):
    d = xall.shape[1]
    kdim = o.shape[1]
    return pl.pallas_call(
        functools.partial(_oproj_kernel, tm=tm, seq=seq),
        grid=(rows // tm,),
        in_specs=[
            pl.BlockSpec((tm, kdim), lambda i: (i, 0)),
            _const_spec((kdim, d)),
            pl.BlockSpec((tm, d), lambda i: (i, 0)),
            _const_spec(mods.shape),
        ],
        out_specs=pl.BlockSpec((tm, d), lambda i: (i, 0)),
        out_shape=jax.ShapeDtypeStruct((rows, d), F32),
        compiler_params=_cparams(("arbitrary",)),
        name="oproj",
    )(o, w_o, xall, mods)


def _mlp_kernel(x_ref, mod_ref, n2_ref, w1_ref, w2_ref, fn_ref, out_ref, h_sc, acc_sc, *, tm, seq, final_norm):
    row0 = pl.program_id(0) * tm
    kf = pl.program_id(1)

    @pl.when(kf == 0)
    def _():
        h_sc[...] = _norm_mod(x_ref[...], n2_ref[...], mod_ref, 3, 4, row0, tm, seq).astype(BF16)
        acc_sc[...] = jnp.zeros_like(acc_sc)

    a = jnp.maximum(_dot(h_sc[...], w1_ref[...]), 0.0)
    acc_sc[...] += _dot((a * a).astype(BF16), w2_ref[...])

    @pl.when(kf == pl.num_programs(1) - 1)
    def _():
        y = x_ref[...] + _row_vec(mod_ref, 5, row0, tm, seq) * acc_sc[...]
        out_ref[...] = _rms(y, fn_ref[...]) if final_norm else y


def _mlp(xall, mods, n2, w1, w2, fnorm, *, rows, seq, tm, tf=512, final_norm=False):
    d = xall.shape[1]
    dff = w1.shape[1]
    return pl.pallas_call(
        functools.partial(_mlp_kernel, tm=tm, seq=seq, final_norm=final_norm),
        grid=(rows // tm, dff // tf),
        in_specs=[
            pl.BlockSpec((tm, d), lambda i, k: (i, 0)),
            _const_spec(mods.shape),
            _const_spec((1, d)),
            pl.BlockSpec((d, tf), lambda i, k: (0, k)),
            pl.BlockSpec((tf, d), lambda i, k: (k, 0)),
            _const_spec((1, d)),
        ],
        out_specs=pl.BlockSpec((tm, d), lambda i, k: (i, 0)),
        out_shape=jax.ShapeDtypeStruct((rows, d), F32),
        scratch_shapes=[pltpu.VMEM((tm, d), BF16), pltpu.VMEM((tm, d), F32)],
        compiler_params=_cparams(("arbitrary", "arbitrary")),
        name="mlp",
    )(xall, mods, n2, w1, w2, fnorm)


def _conv_in_kernel(x_ref, mod_ref, n1_ref, wb_ref, wc_ref, wu_ref, bg_ref, v_ref, h_sc, *, tm, seq):
    i = pl.program_id(0)

    @pl.when(pl.program_id(1) == 0)
    def _():
        h_sc[...] = _norm_mod(x_ref[...], n1_ref[...], mod_ref, 0, 1, i * tm, tm, seq).astype(BF16)

    h = h_sc[...]
    bg_ref[...] = _dot(h, wb_ref[...])
    v_ref[...] = _dot(h, wc_ref[...]) * _dot(h, wu_ref[...])


def _conv_in(xall, mods, n1, w_in, *, seq, tm=768, tn=512):
    t, d = xall.shape
    nb = d // tn
    out = jax.ShapeDtypeStruct((t, d), F32)
    return pl.pallas_call(
        functools.partial(_conv_in_kernel, tm=tm, seq=seq),
        grid=(t // tm, nb),
        in_specs=[
            pl.BlockSpec((tm, d), lambda i, j: (i, 0)),
            _const_spec(mods.shape),
            _const_spec((1, d)),
            pl.BlockSpec((d, tn), lambda i, j: (0, j)),
            pl.BlockSpec((d, tn), lambda i, j: (0, nb + j)),
            pl.BlockSpec((d, tn), lambda i, j: (0, 2 * nb + j)),
        ],
        out_specs=[pl.BlockSpec((tm, tn), lambda i, j: (i, j))] * 2,
        out_shape=[out, out],
        scratch_shapes=[pltpu.VMEM((tm, d), BF16)],
        compiler_params=_cparams(("arbitrary", "arbitrary")),
        name="conv_in",
    )(xall, mods, n1, w_in, w_in, w_in)


def _conv_out_kernel(v_ref, vp_ref, vn_ref, bg_ref, cw_ref, w_ref, x_ref, mod_ref, out_ref, *, tm, seq, total):
    row0 = pl.program_id(0) * tm
    v = v_ref[...]
    rows = row0 + lax.broadcasted_iota(jnp.int32, (tm, 1), 0)
    local = lax.broadcasted_iota(jnp.int32, (tm, 1), 0)
    prev = jnp.where(local == 0, vp_ref[SUBLANES - 1:SUBLANES, :], pltpu.roll(v, 1, axis=0))
    nxt = jnp.where(local == tm - 1, vn_ref[0:1, :], pltpu.roll(v, tm - 1, axis=0))
    prev = jnp.where((rows == 0) | (rows == seq), 0.0, prev)
    nxt = jnp.where((rows == seq - 1) | (rows == total - 1), 0.0, nxt)
    cw = cw_ref[...]
    conv = cw[0:1, :] * prev + cw[1:2, :] * v + cw[2:3, :] * nxt
    y = _dot((bg_ref[...] * conv).astype(BF16), w_ref[...])
    out_ref[...] = x_ref[...] + _row_vec(mod_ref, 2, row0, tm, seq) * y


def _conv_out(v, bg, conv_w, w_out, xall, mods, *, seq, tm=384):
    t, d = xall.shape
    hb = tm // SUBLANES
    last = t // SUBLANES - 1
    return pl.pallas_call(
        functools.partial(_conv_out_kernel, tm=tm, seq=seq, total=t),
        grid=(t // tm,),
        in_specs=[
            pl.BlockSpec((tm, d), lambda i: (i, 0)),
            pl.BlockSpec((SUBLANES, d), lambda i: (jnp.maximum(i * hb - 1, 0), 0)),
            pl.BlockSpec((SUBLANES, d), lambda i: (jnp.minimum((i + 1) * hb, last), 0)),
            pl.BlockSpec((tm, d), lambda i: (i, 0)),
            _const_spec((CONV_WIDTH, d)),
            _const_spec((d, d)),
            pl.BlockSpec((tm, d), lambda i: (i, 0)),
            _const_spec(mods.shape),
        ],
        out_specs=pl.BlockSpec((tm, d), lambda i: (i, 0)),
        out_shape=jax.ShapeDtypeStruct((t, d), F32),
        compiler_params=_cparams(("arbitrary",)),
        name="conv_out",
    )(v, v, v, bg, conv_w, w_out, xall, mods)


def _mla_proj_kernel(x_ref, mod_ref, n1_ref, wdq_ref, qg_ref, wuq_ref, wdkv_ref, kvg_ref, wukv_ref,
                     cos_ref, sa_ref, sb_ref, q_out, k_out, v_out, *, tm, seq, q_scale):
    row0 = pl.program_id(0) * tm
    h = _norm_mod(x_ref[...], n1_ref[...], mod_ref, 0, 1, row0, tm, seq).astype(BF16)
    cs, sa, sb = cos_ref[...], sa_ref[...], sb_ref[...]

    def rope(z):
        return z * cs + pltpu.roll(z, LANES - C_ROPE // 2, axis=1) * sa + pltpu.roll(z, C_ROPE // 2, axis=1) * sb

    qn = _rms(_dot(h, wdq_ref[...]), qg_ref[...] * q_scale).astype(BF16)
    q = _dot(qn, wuq_ref[...])
    ckr = _dot(h, wdkv_ref[...])
    rank = kvg_ref.shape[1]
    ckv = _rms(ckr[:, :rank], kvg_ref[...]).astype(BF16)
    kr = rope(ckr[:, rank:]).astype(BF16)
    kv = _dot(ckv, wukv_ref[...])
    for hh in range(C_HEADS):
        b = hh * C_QK_PAD
        q_out[:, b:b + C_NOPE] = q[:, b:b + C_NOPE].astype(BF16)
        q_out[:, b + C_NOPE:b + C_QK_PAD] = rope(q[:, b + C_NOPE:b + C_QK_PAD]).astype(BF16)
        k_out[:, b:b + C_NOPE] = kv[:, b:b + C_NOPE].astype(BF16)
        k_out[:, b + C_NOPE:b + C_QK_PAD] = kr
        v_out[:, hh * C_V:(hh + 1) * C_V] = kv[:, b + C_NOPE:b + C_QK_PAD].astype(BF16)


def _mla_proj(xall, mods, n1, w_dq, q_gain, w_uq, w_dkv, kv_gain, w_ukv, cos, sa, sb, *, seq, tm=256):
    t, d = xall.shape
    q_scale = (C_NOPE + C_ROPE) ** -0.5 * LOG2E
    row = lambda i: (i, 0)
    return pl.pallas_call(
        functools.partial(_mla_proj_kernel, tm=tm, seq=seq, q_scale=q_scale),
        grid=(t // tm,),
        in_specs=[
            pl.BlockSpec((tm, d), row),
            _const_spec(mods.shape),
            _const_spec((1, d)),
            _const_spec(w_dq.shape),
            _const_spec(q_gain.shape),
            _const_spec(w_uq.shape),
            _const_spec(w_dkv.shape),
            _const_spec(kv_gain.shape),
            _const_spec(w_ukv.shape),
            pl.BlockSpec((tm, LANES), row),
            pl.BlockSpec((tm, LANES), row),
            pl.BlockSpec((tm, LANES), row),
        ],
        out_specs=[
            pl.BlockSpec((tm, C_HEADS * C_QK_PAD), row),
            pl.BlockSpec((tm, C_HEADS * C_QK_PAD), row),
            pl.BlockSpec((tm, C_HEADS * C_V), row),
        ],
        out_shape=[
            jax.ShapeDtypeStruct((t, C_HEADS * C_QK_PAD), BF16),
            jax.ShapeDtypeStruct((t, C_HEADS * C_QK_PAD), BF16),
            jax.ShapeDtypeStruct((t, C_HEADS * C_V), BF16),
        ],
        compiler_params=_cparams(("arbitrary",)),
        name="mla_proj",
    )(xall, mods, n1, w_dq, q_gain, w_uq, w_dkv, kv_gain, w_ukv, cos, sa, sb)


def _rope_tables(seq, n_ctx, rot_dim):
    pos = jnp.arange(seq, dtype=jnp.int32)
    row = (pos // GRID_W).astype(F32)
    col = (pos % GRID_W).astype(F32)
    n_axis = rot_dim // 4
    inv = ROPE_THETA ** (-jnp.arange(n_axis, dtype=F32) / n_axis)
    ang = jnp.concatenate([row[:, None] * inv, col[:, None] * inv], axis=-1)
    cos = jnp.concatenate([jnp.cos(ang), jnp.ones((n_ctx, rot_dim // 2), F32)], axis=0)
    sin = jnp.concatenate([jnp.sin(ang), jnp.zeros((n_ctx, rot_dim // 2), F32)], axis=0)
    return cos, sin


def kernel(x, c, ctx, c_ctx, w_ada, b_ada, norm1, norm2, w_mlp1, w_mlp2, a_w_qkv, a_q_norm, a_k_norm, a_w_o,
           b_w_in, b_conv, b_w_out, c_w_dq, c_q_norm, c_w_uq, c_w_dkv, c_kv_norm, c_w_ukv, c_w_o, final_norm):
    batch, seq, d = x.shape
    n_ctx = ctx.shape[1]
    depth = w_ada.shape[0]
    assert batch == 1 and seq % 1024 == 0 and n_ctx == 256
    total = seq + n_ctx

    xall = jnp.concatenate([x[0], ctx[0]], axis=0)

    c8 = jnp.zeros((SUBLANES, d), F32).at[0].set(c[0]).at[1].set(c_ctx)
    mods_all = _ada(c8, w_ada, b_ada)[:, :2, :].reshape(depth, 2, 6, d)

    cos_a, sin_a = _rope_tables(seq, n_ctx, A_HEAD_DIM)
    cos_a2 = jnp.concatenate([cos_a, cos_a], axis=1)
    sin_a2 = jnp.concatenate([-sin_a, sin_a], axis=1)
    cos_c, sin_c = _rope_tables(seq, n_ctx, C_ROPE)
    zpad = jnp.zeros((total, LANES - C_ROPE), F32)
    zhalf = jnp.zeros((total, C_ROPE // 2), F32)
    cos_c2 = jnp.concatenate([cos_c, cos_c, zpad], axis=1)
    sin_ca = jnp.concatenate([-sin_c, zhalf, zpad], axis=1)
    sin_cb = jnp.concatenate([zhalf, sin_c, zpad], axis=1)

    out = None
    for i in range(depth):
        last = i == depth - 1
        mixer, j = i % N_MIXERS, i // N_MIXERS
        mods = mods_all[i]
        n1 = norm1[i][None, :]
        n2 = norm2[i][None, :]
        rows = seq if last else total

        if mixer == 0:
            a_scale = A_HEAD_DIM ** -0.5 * LOG2E
            gains = jnp.concatenate([
                jnp.broadcast_to(a_q_norm[j] * a_scale, (A_HEADS * A_HEAD_DIM // 512, A_HEAD_DIM)),
                jnp.broadcast_to(a_k_norm[j], (A_KV_HEADS * A_HEAD_DIM // 512, A_HEAD_DIM)),
                jnp.ones((A_KV_HEADS * A_HEAD_DIM // 512, A_HEAD_DIM), F32),
            ], axis=0)[:, None, :]
            qkv = _gqa_qkv(xall, mods, n1, a_w_qkv[j].astype(BF16), gains, cos_a2, sin_a2, seq=seq)
            o = _flash(qkv, qkv, qkv, n_kv=A_KV_HEADS, group=A_GROUP, dk=A_HEAD_DIM, dv=A_HEAD_DIM,
                       q_rows=rows, q_col0=0, k_col0=A_HEADS * A_HEAD_DIM,
                       v_col0=(A_HEADS + A_KV_HEADS) * A_HEAD_DIM, seq=seq, n_ctx=n_ctx)
            xall = _oproj(o, a_w_o[j].astype(BF16), xall, mods, rows=rows, seq=seq, tm=256 if last else 384)
        elif mixer == 1:
            bg, v = _conv_in(xall, mods, n1, b_w_in[j].astype(BF16), seq=seq)
            xall = _conv_out(v, bg, b_conv[j], b_w_out[j].astype(BF16), xall, mods, seq=seq)
        else:
            w_uq = jnp.pad(c_w_uq[j].reshape(-1, C_HEADS, C_NOPE + C_ROPE),
                           ((0, 0), (0, 0), (0, C_QK_PAD - C_NOPE - C_ROPE))).reshape(-1, C_HEADS * C_QK_PAD)
            w_dkv = jnp.pad(c_w_dkv[j], ((0, 0), (0, LANES - C_ROPE)))
            q, k, v = _mla_proj(xall, mods, n1, c_w_dq[j].astype(BF16), c_q_norm[j][None, :], w_uq.astype(BF16),
                                w_dkv.astype(BF16), c_kv_norm[j][None, :], c_w_ukv[j].astype(BF16),
                                cos_c2, sin_ca, sin_cb, seq=seq)
            o = _flash(q, k, v, n_kv=C_HEADS, group=1, dk=C_QK_PAD, dv=C_V, q_rows=rows,
                       q_col0=0, k_col0=0, v_col0=0, seq=seq, n_ctx=n_ctx)
            xall = _oproj(o, c_w_o[j].astype(BF16), xall, mods, rows=rows, seq=seq, tm=256 if last else 384)

        xall = _mlp(xall, mods, n2, w_mlp1[i].astype(BF16), w_mlp2[i].astype(BF16), final_norm[None, :],
                    rows=rows, seq=seq, tm=512 if last else 768, final_norm=last)
    return xall[None]
```
